```python
import math
import jax, jax.numpy as jnp
from jax import lax
import numpy as np

D_MODEL = 1024
BATCH = 4
SEQ = 4096
DEPTH = 1
DEC_BATCH = 128
DEC_SEQ = 8
PAST_LEN = 8192
PAGE_SIZE = 128

SSM_WIDTH = D_MODEL // 2
SSM_GROUP = 16
SSM_GROUPS = SSM_WIDTH // SSM_GROUP
SSM_STATE = 64
N_HEADS = 8
N_KV_HEADS = 2
Q_PER_KV = N_HEADS // N_KV_HEADS
HEAD_DIM = 64
ATTN_WIDTH = N_HEADS * HEAD_DIM
KV_WIDTH = N_KV_HEADS * HEAD_DIM
WINDOW = 128
ROPE_DIM = HEAD_DIM // 4
ROPE_THETA = 500000.0
SPLITS = [SSM_WIDTH, SSM_WIDTH + ATTN_WIDTH, SSM_WIDTH + ATTN_WIDTH + KV_WIDTH,
          SSM_WIDTH + ATTN_WIDTH + 2 * KV_WIDTH, SSM_WIDTH + ATTN_WIDTH + 2 * KV_WIDTH + D_MODEL]
IN_WIDTH = SSM_WIDTH + ATTN_WIDTH + 2 * KV_WIDTH + 2 * D_MODEL
PEER_HEADS = 8
PEER_KEYS = 128
PEER_EXPERTS = PEER_KEYS * PEER_KEYS
PEER_KEY_DIM = 128
PEER_HALF = PEER_KEY_DIM // 2
PEER_TOPK = 16
PEER_BLOCK = 128
ALPHA = (2.0 * DEPTH) ** 0.25
BETA = (8.0 * DEPTH) ** -0.25
LN_EPS = 1e-5
NEG_INF = -1e30

kernel_name = "hybrid_s5_swa_peer_deepnorm_step"


def layer_norm(x, g, b):
    xf = x.astype(jnp.float32)
    mu = jnp.mean(xf, axis=-1, keepdims=True)
    var = jnp.mean(jnp.square(xf - mu), axis=-1, keepdims=True)
    return ((xf - mu) * lax.rsqrt(var + LN_EPS) * g.astype(jnp.float32) + b.astype(jnp.float32)).astype(x.dtype)


def partial_rope(x, pos):
    freqs = ROPE_THETA ** (-jnp.arange(0, ROPE_DIM, 2, dtype=jnp.float32) / ROPE_DIM)
    ang = pos.astype(jnp.float32)[:, None] * freqs[None, :]
    cos = jnp.cos(ang)[:, None, :]
    sin = jnp.sin(ang)[:, None, :]
    xr = x[..., :ROPE_DIM].astype(jnp.float32)
    x1, x2 = xr[..., :ROPE_DIM // 2], xr[..., ROPE_DIM // 2:]
    rot = jnp.concatenate([x1 * cos - x2 * sin, x2 * cos + x1 * sin], axis=-1)
    return jnp.concatenate([rot.astype(x.dtype), x[..., ROPE_DIM:]], axis=-1)


def s5_scan(u, h0_re, h0_im, a_re, a_im, log_dt, b_re, b_im, c_re, c_im, d):
    f32 = jnp.float32
    bsz, t = u.shape[0], u.shape[1]
    dt = jnp.exp(log_dt.astype(f32))[:, None]
    a = lax.complex(a_re.astype(f32), a_im.astype(f32))
    a_bar = jnp.exp(a * dt)
    b = lax.complex(b_re.astype(f32), b_im.astype(f32))
    b_bar = ((a_bar - 1.0) / a)[..., None] * b
    ug = u.astype(f32).reshape(bsz, t, SSM_GROUPS, SSM_GROUP).astype(jnp.complex64)
    bu = jnp.einsum('gpc,btgc->btgp', b_bar, ug)
    h0 = lax.complex(h0_re.astype(f32), h0_im.astype(f32))
    bu = bu.at[:, 0].add(a_bar[None] * h0)
    decay = jnp.broadcast_to(a_bar, bu.shape)

    def combine(l, r):
        return (r[0] * l[0], r[0] * l[1] + r[1])

    _, hs = lax.associative_scan(combine, (decay, bu), axis=1)
    c = lax.complex(c_re.astype(f32), c_im.astype(f32))
    y = jnp.real(jnp.einsum('gcp,btgp->btgc', c, hs)).reshape(bsz, t, SSM_WIDTH)
    y = y + d.astype(f32) * u.astype(f32)
    h_last = hs[:, -1]
    return y.astype(u.dtype), jnp.real(h_last), jnp.imag(h_last)


def sink_attention(q, k, v, mask, sinks):
    f32 = jnp.float32
    s = jnp.einsum('bnqkgd,bnskd->bnkgqs', q.astype(f32), k.astype(f32)) * (HEAD_DIM ** -0.5)
    s = jnp.where(mask[None, :, None, None], s, NEG_INF)
    sink = jnp.broadcast_to(sinks.astype(f32).reshape(N_KV_HEADS, Q_PER_KV)[None, None, :, :, None, None],
                            s.shape[:-1] + (1,))
    p = jax.nn.softmax(jnp.concatenate([s, sink], axis=-1), axis=-1)[..., :-1]
    o = jnp.einsum('bnkgqs,bnskd->bnqkgd', p, v.astype(f32))
    return o.astype(q.dtype)


def window_attention_prompt(q, k, v, sinks):
    bsz, t = q.shape[0], q.shape[1]
    nb = t // WINDOW
    qb = q.reshape(bsz, nb, WINDOW, N_KV_HEADS, Q_PER_KV, HEAD_DIM)
    pad = ((0, 0), (WINDOW, 0), (0, 0), (0, 0))
    kp = jnp.pad(k, pad).reshape(bsz, nb + 1, WINDOW, N_KV_HEADS, HEAD_DIM)
    vp = jnp.pad(v, pad).reshape(bsz, nb + 1, WINDOW, N_KV_HEADS, HEAD_DIM)
    kb = jnp.concatenate([kp[:, :-1], kp[:, 1:]], axis=2)
    vb = jnp.concatenate([vp[:, :-1], vp[:, 1:]], axis=2)
    qi = jnp.arange(WINDOW)[:, None]
    sj = jnp.arange(2 * WINDOW)[None, :]
    rel = WINDOW + qi - sj
    band = (rel >= 0) & (rel < WINDOW)
    valid = (jnp.arange(nb)[:, None, None] > 0) | (sj[None] >= WINDOW)
    mask = band[None] & valid
    o = sink_attention(qb, kb, vb, mask, sinks)
    return o.reshape(bsz, t, ATTN_WIDTH), k[:, t - WINDOW:], v[:, t - WINDOW:]


def window_attention_sample(q, k_new, v_new, k_buf, v_buf, sinks):
    bsz, s = q.shape[0], q.shape[1]
    kk = jnp.concatenate([k_buf.astype(k_new.dtype), k_new], axis=1)
    vv = jnp.concatenate([v_buf.astype(v_new.dtype), v_new], axis=1)
    rel = (jnp.arange(s)[:, None] + WINDOW) - jnp.arange(WINDOW + s)[None, :]
    mask = (rel >= 0) & (rel < WINDOW)
    qb = q.reshape(bsz, 1, s, N_KV_HEADS, Q_PER_KV, HEAD_DIM)
    o = sink_attention(qb, kk[:, None], vv[:, None], mask[None], sinks)
    return o.reshape(bsz, s, ATTN_WIDTH), kk[:, s:], vv[:, s:]


def token_mixer(x, pos, h0_re, h0_im, k_buf, v_buf, p):
    bsz, t = x.shape[0], x.shape[1]
    z = x @ p['w_in'] + p['b_in']
    u, q, k, v, g_ssm, g_attn = jnp.split(z, SPLITS, axis=-1)
    y, h_re, h_im = s5_scan(u, h0_re, h0_im, p['ssm_a_re'], p['ssm_a_im'], p['ssm_log_dt'],
                            p['ssm_b_re'], p['ssm_b_im'], p['ssm_c_re'], p['ssm_c_im'], p['ssm_d'])
    hg = jax.nn.gelu(y, approximate=False)
    ga, gb = jnp.split(hg @ p['w_glu'] + p['b_glu'], 2, axis=-1)
    ssm_branch = ga * jax.nn.sigmoid(gb)
    q = partial_rope(q.reshape(bsz, t, N_HEADS, HEAD_DIM), pos)
    k = partial_rope(k.reshape(bsz, t, N_KV_HEADS, HEAD_DIM), pos)
    v = v.reshape(bsz, t, N_KV_HEADS, HEAD_DIM)
    if k_buf is None:
        o, k_win, v_win = window_attention_prompt(q, k, v, p['attn_sinks'])
    else:
        o, k_win, v_win = window_attention_sample(q, k, v, k_buf, v_buf, p['attn_sinks'])
    attn_branch = o @ p['w_attn_up']
    merged = jax.nn.sigmoid(g_ssm) * ssm_branch + jax.nn.sigmoid(g_attn) * attn_branch
    return merged @ p['w_out'], k_win, v_win, h_re, h_im


def peer(x, p):
    f32 = jnp.float32
    bsz, t, dm = x.shape
    xf = x.reshape(-1, dm)
    n = xf.shape[0]
    q = (xf @ p['peer_wq']).astype(f32).reshape(n, PEER_HEADS, 2, PEER_HALF)
    s1 = jnp.einsum('nhd,hkd->nhk', q[:, :, 0], p['peer_k1'].astype(f32))
    s2 = jnp.einsum('nhd,hkd->nhk', q[:, :, 1], p['peer_k2'].astype(f32))
    v1, i1 = lax.top_k(s1, PEER_TOPK)
    v2, i2 = lax.top_k(s2, PEER_TOPK)
    cand = (v1[..., :, None] + v2[..., None, :]).reshape(n, PEER_HEADS, PEER_TOPK * PEER_TOPK)
    cidx = (i1[..., :, None] * PEER_KEYS + i2[..., None, :]).reshape(n, PEER_HEADS, PEER_TOPK * PEER_TOPK)
    sc, sel = lax.top_k(cand, PEER_TOPK)
    idx = jnp.take_along_axis(cidx, sel, axis=-1)
    gate = jax.nn.softmax(sc, axis=-1)
    n_pad = (-n) % PEER_BLOCK
    nblk = (n + n_pad) // PEER_BLOCK
    xb = jnp.pad(xf, ((0, n_pad), (0, 0))).reshape(nblk, PEER_BLOCK, dm)
    ib = jnp.pad(idx, ((0, n_pad), (0, 0), (0, 0))).reshape(nblk, PEER_BLOCK, PEER_HEADS, PEER_TOPK)
    gbk = jnp.pad(gate, ((0, n_pad), (0, 0), (0, 0))).reshape(nblk, PEER_BLOCK, PEER_HEADS, PEER_TOPK)
    peer_u, peer_v = p['peer_u'], p['peer_v']

    def expert_block(args):
        xi, ii, gi = args
        h = jnp.einsum('nhkd,nd->nhk', peer_u[ii], xi)
        a = jax.nn.gelu(h.astype(f32), approximate=False) * gi
        return jnp.einsum('nhk,nhkd->nd', a.astype(x.dtype), peer_v[ii])

    out = lax.map(expert_block, (xb, ib, gbk)).reshape(-1, dm)[:n]
    return out.reshape(bsz, t, dm).astype(x.dtype)


def decoder_layer(x, pos, h0_re, h0_im, k_buf, v_buf, p):
    mix, k_win, v_win, h_re, h_im = token_mixer(x, pos, h0_re, h0_im, k_buf, v_buf, p)
    x1 = layer_norm(ALPHA * x + mix, p['ln1_g'], p['ln1_b'])
    x2 = layer_norm(ALPHA * x1 + peer(x1, p), p['ln2_g'], p['ln2_b'])
    return x2, k_win, v_win, h_re, h_im


def setup_inputs(seed: int = 0) -> dict:
    key = jax.random.key(seed)
    ks = jax.random.split(key, 32)
    nrm = lambda k, shape, s: jax.random.normal(k, shape, jnp.float32) * s
    L = DEPTH
    a_im_base = math.pi * jnp.arange(SSM_STATE, dtype=jnp.float32)
    return {
        'x_prompt': nrm(ks[0], (BATCH, SEQ, D_MODEL), 1.0),
        'x_sample': nrm(ks[1], (DEC_BATCH, DEC_SEQ, D_MODEL), 1.0),
        'cache_k_win': nrm(ks[2], (L, DEC_BATCH, WINDOW, N_KV_HEADS, HEAD_DIM), 1.0),
        'cache_v_win': nrm(ks[3], (L, DEC_BATCH, WINDOW, N_KV_HEADS, HEAD_DIM), 1.0),
        'state_ssm_re': nrm(ks[4], (L, DEC_BATCH, SSM_GROUPS, SSM_STATE), 0.1),
        'state_ssm_im': nrm(ks[5], (L, DEC_BATCH, SSM_GROUPS, SSM_STATE), 0.1),
        'w_in': nrm(ks[6], (L, D_MODEL, IN_WIDTH), D_MODEL ** -0.5),
        'b_in': nrm(ks[7], (L, IN_WIDTH), 0.01),
        'ssm_a_re': -0.5 + nrm(ks[8], (L, SSM_GROUPS, SSM_STATE), 0.01),
        'ssm_a_im': a_im_base + nrm(ks[9], (L, SSM_GROUPS, SSM_STATE), 0.01),
        'ssm_log_dt': jax.random.uniform(ks[10], (L, SSM_GROUPS), jnp.float32, math.log(1e-3), math.log(1e-1)),
        'ssm_b_re': nrm(ks[11], (L, SSM_GROUPS, SSM_STATE, SSM_GROUP), (2.0 * SSM_GROUP) ** -0.5),
        'ssm_b_im': nrm(ks[12], (L, SSM_GROUPS, SSM_STATE, SSM_GROUP), (2.0 * SSM_GROUP) ** -0.5),
        'ssm_c_re': nrm(ks[13], (L, SSM_GROUPS, SSM_GROUP, SSM_STATE), SSM_STATE ** -0.5),
        'ssm_c_im': nrm(ks[14], (L, SSM_GROUPS, SSM_GROUP, SSM_STATE), SSM_STATE ** -0.5),
        'ssm_d': nrm(ks[15], (L, SSM_WIDTH), 1.0),
        'w_glu': nrm(ks[16], (L, SSM_WIDTH, 2 * D_MODEL), SSM_WIDTH ** -0.5),
        'b_glu': nrm(ks[17], (L, 2 * D_MODEL), 0.01),
        'attn_sinks': nrm(ks[18], (L, N_HEADS), 0.5),
        'w_attn_up': nrm(ks[19], (L, ATTN_WIDTH, D_MODEL), ATTN_WIDTH ** -0.5),
        'w_out': nrm(ks[20], (L, D_MODEL, D_MODEL), BETA * D_MODEL ** -0.5),
        'ln1_g': 1.0 + nrm(ks[21], (L, D_MODEL), 0.01),
        'ln1_b': nrm(ks[22], (L, D_MODEL), 0.01),
        'peer_wq': nrm(ks[23], (L, D_MODEL, PEER_HEADS * PEER_KEY_DIM), D_MODEL ** -0.5),
        'peer_k1': nrm(ks[24], (L, PEER_HEADS, PEER_KEYS, PEER_HALF), PEER_HALF ** -0.5),
        'peer_k2': nrm(ks[25], (L, PEER_HEADS, PEER_KEYS, PEER_HALF), PEER_HALF ** -0.5),
        'peer_u': nrm(ks[26], (L, PEER_EXPERTS, D_MODEL), D_MODEL ** -0.5),
        'peer_v': nrm(ks[27], (L, PEER_EXPERTS, D_MODEL), BETA * D_MODEL ** -0.5),
        'ln2_g': 1.0 + nrm(ks[28], (L, D_MODEL), 0.01),
        'ln2_b': nrm(ks[29], (L, D_MODEL), 0.01),
    }


def reference(x_prompt, x_sample, cache_k_win, cache_v_win, state_ssm_re, state_ssm_im,
              w_in, b_in, ssm_a_re, ssm_a_im, ssm_log_dt, ssm_b_re, ssm_b_im, ssm_c_re, ssm_c_im,
              ssm_d, w_glu, b_glu, attn_sinks, w_attn_up, w_out, ln1_g, ln1_b,
              peer_wq, peer_k1, peer_k2, peer_u, peer_v, ln2_g, ln2_b):
    pos_prompt = jnp.arange(SEQ, dtype=jnp.int32)
    pos_sample = PAST_LEN + jnp.arange(DEC_SEQ, dtype=jnp.int32)
    zero_state = jnp.zeros((BATCH, SSM_GROUPS, SSM_STATE), jnp.float32)
    xp, xs = x_prompt, x_sample
    kwp, vwp, hrp, hip, kws, vws, hrs, his = [], [], [], [], [], [], [], []
    for l in range(DEPTH):
        p = {
            'w_in': w_in[l], 'b_in': b_in[l], 'ssm_a_re': ssm_a_re[l], 'ssm_a_im': ssm_a_im[l],
            'ssm_log_dt': ssm_log_dt[l], 'ssm_b_re': ssm_b_re[l], 'ssm_b_im': ssm_b_im[l],
            'ssm_c_re': ssm_c_re[l], 'ssm_c_im': ssm_c_im[l], 'ssm_d': ssm_d[l],
            'w_glu': w_glu[l], 'b_glu': b_glu[l], 'attn_sinks': attn_sinks[l],
            'w_attn_up': w_attn_up[l], 'w_out': w_out[l], 'ln1_g': ln1_g[l], 'ln1_b': ln1_b[l],
            'peer_wq': peer_wq[l], 'peer_k1': peer_k1[l], 'peer_k2': peer_k2[l],
            'peer_u': peer_u[l], 'peer_v': peer_v[l], 'ln2_g': ln2_g[l], 'ln2_b': ln2_b[l],
        }
        xp, kw, vw, hr, hi = decoder_layer(xp, pos_prompt, zero_state, zero_state, None, None, p)
        kwp.append(kw); vwp.append(vw); hrp.append(hr); hip.append(hi)
        xs, kw, vw, hr, hi = decoder_layer(xs, pos_sample, state_ssm_re[l], state_ssm_im[l],
                                           cache_k_win[l], cache_v_win[l], p)
        kws.append(kw); vws.append(vw); hrs.append(hr); his.append(hi)
    return (xp, xs, jnp.stack(kwp), jnp.stack(vwp), jnp.stack(hrp), jnp.stack(hip),
            jnp.stack(kws), jnp.stack(vws), jnp.stack(hrs), jnp.stack(his))
```

```python
import functools
import math

import numpy as np
import jax
import jax.numpy as jnp
from jax import lax
from jax.experimental import pallas as pl
from jax.experimental.pallas import tpu as pltpu

F32 = jnp.float32
BF16 = jnp.bfloat16

SSM_GROUP = 16
HEAD_DIM = 64
ROPE_DIM = HEAD_DIM // 4
ROPE_THETA = 500000.0
PAST_LEN = 8192
PEER_TOPK = 16
LN_EPS = 1e-5

LANES = 128
VMEM_LIMIT = 56 * 1024 * 1024


def _cparams(sem):
    return pltpu.CompilerParams(dimension_semantics=sem, vmem_limit_bytes=VMEM_LIMIT)


def _gelu(x):
    return 0.5 * x * (1.0 + lax.erf(x * (2.0 ** -0.5)))


def _sigmoid(x):
    return 1.0 / (1.0 + jnp.exp(-x))


def _layer_norm(r, g, b):
    mu = jnp.mean(r, axis=-1, keepdims=True)
    d = r - mu
    var = jnp.mean(d * d, axis=-1, keepdims=True)
    return d * lax.rsqrt(var + LN_EPS) * g + b


def _in_proj_kernel(x_ref, w_ref, b_ref, cos_ref, sin_ref,
                    u_ref, q_ref, k_ref, v_ref, gs_ref, ga_ref, *, widths):
    ssm_w, attn_w, kv_w, d_model = widths
    z = jnp.dot(x_ref[...].astype(BF16), w_ref[...], preferred_element_type=F32) + b_ref[...]
    o_q = ssm_w
    o_k = o_q + attn_w
    o_v = o_k + kv_w
    o_gs = o_v + kv_w
    o_ga = o_gs + d_model
    u_ref[...] = z[:, :ssm_w]
    v_ref[...] = z[:, o_v:o_gs]
    gs_ref[...] = z[:, o_gs:o_ga]
    ga_ref[...] = z[:, o_ga:]

    cos = cos_ref[...]
    sin = sin_ref[...]
    lane = lax.broadcasted_iota(jnp.int32, cos.shape, 1)
    first_half = (lane % HEAD_DIM) < (ROPE_DIM // 2)

    def rope(x):
        partner = jnp.where(first_half, pltpu.roll(x, LANES - ROPE_DIM // 2, 1),
                            pltpu.roll(x, ROPE_DIM // 2, 1))
        return x * cos + partner * sin

    for s in range(attn_w // LANES):
        q_ref[:, s * LANES:(s + 1) * LANES] = rope(z[:, o_q + s * LANES:o_q + (s + 1) * LANES])
    for s in range(kv_w // LANES):
        k_ref[:, s * LANES:(s + 1) * LANES] = rope(z[:, o_k + s * LANES:o_k + (s + 1) * LANES])


def _rope_tables(positions):
    freqs = ROPE_THETA ** (-np.arange(0, ROPE_DIM, 2, dtype=np.float64) / ROPE_DIM)
    ang = np.asarray(positions, np.float64)[:, None] * freqs[None, :]
    half = ROPE_DIM // 2
    cos = np.ones((len(positions), HEAD_DIM), np.float64)
    sin = np.zeros((len(positions), HEAD_DIM), np.float64)
    cos[:, :half] = np.cos(ang)
    cos[:, half:ROPE_DIM] = np.cos(ang)
    sin[:, :half] = -np.sin(ang)
    sin[:, half:ROPE_DIM] = np.sin(ang)
    reps = LANES // HEAD_DIM
    return (jnp.asarray(np.tile(cos, (1, reps)), F32), jnp.asarray(np.tile(sin, (1, reps)), F32))


def _in_proj(x2d, w_bf, b_in, cos_t, sin_t, widths, tm):
    n, d_model = x2d.shape
    ssm_w, attn_w, kv_w, _ = widths
    in_w = w_bf.shape[1]
    n_tab = cos_t.shape[0] // tm
    row = lambda i: (i, 0)
    const = lambda i: (0, 0)
    out_w = (ssm_w, attn_w, kv_w, kv_w, d_model, d_model)
    return pl.pallas_call(
        functools.partial(_in_proj_kernel, widths=widths),
        out_shape=[jax.ShapeDtypeStruct((n, w), F32) for w in out_w],
        grid=(n // tm,),
        in_specs=[
            pl.BlockSpec((tm, d_model), row),
            pl.BlockSpec((d_model, in_w), const),
            pl.BlockSpec((1, in_w), const),
            pl.BlockSpec((tm, LANES), lambda i: (i % n_tab, 0)),
            pl.BlockSpec((tm, LANES), lambda i: (i % n_tab, 0)),
        ],
        out_specs=[pl.BlockSpec((tm, w), row) for w in out_w],
        compiler_params=_cparams(("parallel",)),
        name="in_proj",
    )(x2d, w_bf, b_in, cos_t, sin_t)


def _ssm_kernel(u_ref, h0re_ref, h0im_ref, bmat_ref, cre_ref, cim_ref, are_ref, aim_ref, d_ref,
                y_ref, hre_ref, him_ref, bu_ref, hsre_ref, hsim_ref, *, unroll):
    bb, tc, w = u_ref.shape
    ns = are_ref.shape[1]
    t_blk = pl.program_id(1)

    @pl.when(t_blk == 0)
    def _():
        hre_ref[...] = h0re_ref[...]
        him_ref[...] = h0im_ref[...]

    u2 = u_ref[...].reshape(bb * tc, w)
    bu = jnp.dot(u2.astype(BF16), bmat_ref[...], preferred_element_type=F32)
    nt = ns // LANES
    tile = lambda j: slice(j * LANES, (j + 1) * LANES)
    for j in range(2 * nt):
        bu_ref[j] = bu[:, tile(j)]
    a_re = [are_ref[:, tile(j)] for j in range(nt)]
    a_im = [aim_ref[:, tile(j)] for j in range(nt)]

    def step(t, carry):
        h_re, h_im = carry
        rows = pl.ds(t, bb, stride=tc)
        n_re, n_im = [], []
        for j in range(nt):
            r = a_re[j] * h_re[j] - a_im[j] * h_im[j] + bu_ref[j, rows, :]
            i = a_re[j] * h_im[j] + a_im[j] * h_re[j] + bu_ref[nt + j, rows, :]
            hsre_ref[j, rows, :] = r
            hsim_ref[j, rows, :] = i
            n_re.append(r)
            n_im.append(i)
        return tuple(n_re), tuple(n_im)

    init = (tuple(hre_ref[:, tile(j)] for j in range(nt)), tuple(him_ref[:, tile(j)] for j in range(nt)))
    h_re, h_im = lax.fori_loop(0, tc, step, init, unroll=unroll)
    for j in range(nt):
        hre_ref[:, tile(j)] = h_re[j]
        him_ref[:, tile(j)] = h_im[j]
    hs_re = jnp.concatenate([hsre_ref[j] for j in range(nt)], axis=1)
    hs_im = jnp.concatenate([hsim_ref[j] for j in range(nt)], axis=1)
    y = (jnp.dot(hs_re.astype(BF16), cre_ref[...], preferred_element_type=F32)
         + jnp.dot(hs_im.astype(BF16), cim_ref[...], preferred_element_type=F32)
         + d_ref[...] * u2)
    y_ref[...] = y.reshape(bb, tc, w)


def _ssm_scan(u3, h0_re, h0_im, ssm, bb, tc):
    bsz, t, w = u3.shape
    bmat, c_re, c_im, a_re, a_im, d = ssm
    ns = a_re.shape[1]
    const = lambda b, c: (0, 0)
    return pl.pallas_call(
        functools.partial(_ssm_kernel, unroll=min(tc, 8)),
        out_shape=[jax.ShapeDtypeStruct((bsz, t, w), F32),
                   jax.ShapeDtypeStruct((bsz, ns), F32),
                   jax.ShapeDtypeStruct((bsz, ns), F32)],
        grid=(bsz // bb, t // tc),
        in_specs=[
            pl.BlockSpec((bb, tc, w), lambda b, c: (b, c, 0)),
            pl.BlockSpec((bb, ns), lambda b, c: (b, 0)),
            pl.BlockSpec((bb, ns), lambda b, c: (b, 0)),
            pl.BlockSpec(bmat.shape, const),
            pl.BlockSpec(c_re.shape, const),
            pl.BlockSpec(c_im.shape, const),
            pl.BlockSpec((1, ns), const),
            pl.BlockSpec((1, ns), const),
            pl.BlockSpec((1, w), const),
        ],
        out_specs=[pl.BlockSpec((bb, tc, w), lambda b, c: (b, c, 0)),
                   pl.BlockSpec((bb, ns), lambda b, c: (b, 0)),
                   pl.BlockSpec((bb, ns), lambda b, c: (b, 0))],
        scratch_shapes=[pltpu.VMEM((2 * ns // LANES, bb * tc, LANES), F32),
                        pltpu.VMEM((ns // LANES, bb * tc, LANES), F32),
                        pltpu.VMEM((ns // LANES, bb * tc, LANES), F32)],
        compiler_params=_cparams(("parallel", "arbitrary")),
        name="ssm_scan",
    )(u3, h0_re, h0_im, bmat, c_re, c_im, a_re, a_im, d)


def _ssm_params(a_re, a_im, log_dt, b_re, b_im, c_re, c_im, d):
    g, p, grp = b_re.shape
    dt = jnp.exp(log_dt.astype(F32))[:, None]
    mag = jnp.exp(a_re * dt)
    abar_re = mag * jnp.cos(a_im * dt)
    abar_im = mag * jnp.sin(a_im * dt)
    den = a_re * a_re + a_im * a_im
    k_re = ((abar_re - 1.0) * a_re + abar_im * a_im) / den
    k_im = (abar_im * a_re - (abar_re - 1.0) * a_im) / den
    bb_re = k_re[..., None] * b_re - k_im[..., None] * b_im
    bb_im = k_re[..., None] * b_im + k_im[..., None] * b_re
    eye = jnp.eye(g, dtype=F32)
    blk_in = lambda m: jnp.einsum('gpc,gh->gchp', m, eye).reshape(g * grp, g * p)
    blk_out = lambda m: jnp.einsum('gcp,gh->gphc', m, eye).reshape(g * p, g * grp)
    bmat = jnp.concatenate([blk_in(bb_re), blk_in(bb_im)], axis=1).astype(BF16)
    return (bmat, blk_out(c_re).astype(BF16), blk_out(-c_im).astype(BF16),
            abar_re.reshape(1, g * p), abar_im.reshape(1, g * p), d.reshape(1, -1))


def _sink_softmax_pv(s, mask, sink, v_bf):
    s = jnp.where(mask, s, -1e30)
    m = jnp.maximum(jnp.max(s, axis=-1, keepdims=True), sink)
    p = jnp.exp(s - m)
    denom = jnp.sum(p, axis=-1, keepdims=True) + jnp.exp(sink - m)
    o = jnp.dot(p.astype(BF16), v_bf, preferred_element_type=F32)
    return o / denom


def _attn_prompt_kernel(sink_ref, q_ref, kp_ref, kc_ref, vp_ref, vc_ref, o_ref, *, n_heads, n_kv):
    win = q_ref.shape[0]
    i = pl.program_id(1)
    qi = lax.broadcasted_iota(jnp.int32, (win, 2 * win), 0)
    sj = lax.broadcasted_iota(jnp.int32, (win, 2 * win), 1)
    rel = win + qi - sj
    mask = (rel >= 0) & (rel < win) & ((i > 0) | (sj >= win))
    kk = jnp.concatenate([kp_ref[...], kc_ref[...]], axis=0).astype(BF16)
    vv = jnp.concatenate([vp_ref[...], vc_ref[...]], axis=0).astype(BF16)
    q = q_ref[...].astype(BF16)
    per_kv = n_heads // n_kv
    scale = HEAD_DIM ** -0.5
    for h in range(n_heads):
        kh = h // per_kv
        k_h = kk[:, kh * HEAD_DIM:(kh + 1) * HEAD_DIM]
        v_h = vv[:, kh * HEAD_DIM:(kh + 1) * HEAD_DIM]
        q_h = q[:, h * HEAD_DIM:(h + 1) * HEAD_DIM]
        s = lax.dot_general(q_h, k_h, (((1,), (1,)), ((), ())), preferred_element_type=F32) * scale
        o_ref[:, h * HEAD_DIM:(h + 1) * HEAD_DIM] = _sink_softmax_pv(s, mask, sink_ref[h], v_h)


def _attn_prompt(q2d, k2d, v2d, sinks, bsz, t, win, n_heads, n_kv):
    nb = t // win
    aw = q2d.shape[1]
    kw = k2d.shape[1]
    cur = lambda b, i: (b * nb + i, 0)
    prev = lambda b, i: (b * nb + jnp.maximum(i - 1, 0), 0)
    return pl.pallas_call(
        functools.partial(_attn_prompt_kernel, n_heads=n_heads, n_kv=n_kv),
        out_shape=jax.ShapeDtypeStruct(q2d.shape, F32),
        grid=(bsz, nb),
        in_specs=[
            pl.BlockSpec(memory_space=pltpu.SMEM),
            pl.BlockSpec((win, aw), cur),
            pl.BlockSpec((win, kw), prev),
            pl.BlockSpec((win, kw), cur),
            pl.BlockSpec((win, kw), prev),
            pl.BlockSpec((win, kw), cur),
        ],
        out_specs=pl.BlockSpec((win, aw), cur),
        compiler_params=_cparams(("parallel", "arbitrary")),
        name="attn_prompt",
    )(sinks, q2d, k2d, k2d, v2d, v2d)


def _attn_sample_kernel(sink_ref, q_ref, kk_ref, vv_ref, o_ref, *, n_heads, n_kv, win):
    bb, s_len, _ = q_ref.shape
    n_keys = kk_ref.shape[1]
    per_kv = n_heads // n_kv
    rows = per_kv * s_len
    qi = lax.broadcasted_iota(jnp.int32, (rows, n_keys), 0) % s_len
    sj = lax.broadcasted_iota(jnp.int32, (rows, n_keys), 1)
    rel = qi + win - sj
    mask = (rel >= 0) & (rel < win)
    row_head = lax.broadcasted_iota(jnp.int32, (rows, 1), 0) // s_len
    scale = HEAD_DIM ** -0.5
    for b in range(bb):
        q = q_ref[b].astype(BF16)
        kk = kk_ref[b].astype(BF16)
        vv = vv_ref[b].astype(BF16)
        for kh in range(n_kv):
            heads = range(kh * per_kv, (kh + 1) * per_kv)
            q_s = jnp.concatenate([q[:, h * HEAD_DIM:(h + 1) * HEAD_DIM] for h in heads], axis=0)
            sink = jnp.zeros((rows, 1), F32)
            for g, h in enumerate(heads):
                sink = jnp.where(row_head == g, sink_ref[h], sink)
            k_h = kk[:, kh * HEAD_DIM:(kh + 1) * HEAD_DIM]
            v_h = vv[:, kh * HEAD_DIM:(kh + 1) * HEAD_DIM]
            s = lax.dot_general(q_s, k_h, (((1,), (1,)), ((), ())), preferred_element_type=F32) * scale
            o_s = _sink_softmax_pv(s, mask, sink, v_h)
            for g, h in enumerate(heads):
                o_ref[b, :, h * HEAD_DIM:(h + 1) * HEAD_DIM] = o_s[g * s_len:(g + 1) * s_len, :]


def _attn_sample(q3, kk, vv, sinks, win, n_heads, n_kv, bb):
    bsz, s_len, aw = q3.shape
    n_keys, kw = kk.shape[1], kk.shape[2]
    blk = lambda i: (i, 0, 0)
    return pl.pallas_call(
        functools.partial(_attn_sample_kernel, n_heads=n_heads, n_kv=n_kv, win=win),
        out_shape=jax.ShapeDtypeStruct(q3.shape, F32),
        grid=(bsz // bb,),
        in_specs=[
            pl.BlockSpec(memory_space=pltpu.SMEM),
            pl.BlockSpec((bb, s_len, aw), blk),
            pl.BlockSpec((bb, n_keys, kw), blk),
            pl.BlockSpec((bb, n_keys, kw), blk),
        ],
        out_specs=pl.BlockSpec((bb, s_len, aw), blk),
        compiler_params=_cparams(("parallel",)),
        name="attn_sample",
    )(sinks, q3, kk, vv)


def _merge_kernel(x_ref, y_ref, o_ref, gs_ref, ga_ref, wglu_ref, bglu_ref, wup_ref, wout_ref,
                  g1_ref, b1_ref, x1_ref, x1t_ref, *, alpha):
    d_model = x_ref.shape[1]
    glu = jnp.dot(_gelu(y_ref[...]).astype(BF16), wglu_ref[...], preferred_element_type=F32) + bglu_ref[...]
    ssm_branch = glu[:, :d_model] * _sigmoid(glu[:, d_model:])
    attn_branch = jnp.dot(o_ref[...].astype(BF16), wup_ref[...], preferred_element_type=F32)
    merged = _sigmoid(gs_ref[...]) * ssm_branch + _sigmoid(ga_ref[...]) * attn_branch
    mix = jnp.dot(merged.astype(BF16), wout_ref[...], preferred_element_type=F32)
    x1 = _layer_norm(alpha * x_ref[...] + mix, g1_ref[...], b1_ref[...])
    x1_ref[...] = x1
    x1t_ref[...] = x1.T.astype(BF16)


def _merge(x2d, y2d, o2d, gs, ga, w_glu, b_glu, w_up, w_out, g1, b1, alpha, tm):
    n, d_model = x2d.shape
    row = lambda i: (i, 0)
    const = lambda i: (0, 0)
    full = lambda a: pl.BlockSpec(a.shape, const)
    return pl.pallas_call(
        functools.partial(_merge_kernel, alpha=alpha),
        out_shape=[jax.ShapeDtypeStruct((n, d_model), F32), jax.ShapeDtypeStruct((d_model, n), BF16)],
        grid=(n // tm,),
        in_specs=[
            pl.BlockSpec((tm, d_model), row),
            pl.BlockSpec((tm, y2d.shape[1]), row),
            pl.BlockSpec((tm, o2d.shape[1]), row),
            pl.BlockSpec((tm, d_model), row),
            pl.BlockSpec((tm, d_model), row),
            full(w_glu), full(b_glu), full(w_up), full(w_out), full(g1), full(b1),
        ],
        out_specs=[pl.BlockSpec((tm, d_model), row), pl.BlockSpec((d_model, tm), lambda i: (0, i))],
        compiler_params=_cparams(("parallel",)),
        name="merge",
    )(x2d, y2d, o2d, gs, ga, w_glu, b_glu, w_up, w_out, g1, b1)


def _top_values(x, count):
    vals = []
    for _ in range(count):
        m = jnp.max(x, axis=0, keepdims=True)
        vals.append(m)
        x = jnp.where(x == m, -jnp.inf, x)
    return vals


def _route_kernel(x1t_ref, wqt_ref, kkt_ref, t_ref, c_ref, s2_ref, e2_ref, cand_ref, *, n_heads, n_keys):
    topk = PEER_TOPK
    qt = jnp.dot(wqt_ref[...], x1t_ref[...], preferred_element_type=F32)
    st = jnp.dot(kkt_ref[...], qt.astype(BF16), preferred_element_type=F32)
    pairs = [(a, b) for a in range(topk + 1) for b in range(topk + 1) if (a + 1) * (b + 1) <= topk + 1]
    cand_ref[...] = jnp.full(cand_ref.shape, -jnp.inf, F32)
    for h in range(n_heads):
        s1 = st[(2 * h) * n_keys:(2 * h + 1) * n_keys, :]
        s2 = st[(2 * h + 1) * n_keys:(2 * h + 2) * n_keys, :]
        v1 = _top_values(s1, topk + 1)
        v2 = _top_values(s2, topk + 1)
        for r, (a, b) in enumerate(pairs):
            cand_ref[r:r + 1, :] = v1[a] + v2[b]
        best = _top_values(cand_ref[...], topk + 1)
        thr = 0.5 * (best[topk - 1] + best[topk])
        z = jnp.exp(best[0] - best[0])
        for c in best[1:topk]:
            z = z + jnp.exp(c - best[0])
        rows = slice(h * n_keys, (h + 1) * n_keys)
        t_ref[rows, :] = thr - s1
        c_ref[rows, :] = jnp.exp(s1 - v1[0]) / z
        s2_ref[rows, :] = s2
        e2_ref[rows, :] = jnp.exp(s2 - v2[0])


def _route(x1t, wqt, kkt, n_heads, n_keys, tb):
    d_model, n = x1t.shape
    hk = n_heads * n_keys
    const = lambda i: (0, 0)
    col = lambda i: (0, i)
    n_cand = 56
    return pl.pallas_call(
        functools.partial(_route_kernel, n_heads=n_heads, n_keys=n_keys),
        out_shape=[jax.ShapeDtypeStruct((hk, n), F32)] * 4,
        grid=(n // tb,),
        in_specs=[pl.BlockSpec((d_model, tb), col),
                  pl.BlockSpec(wqt.shape, const),
                  pl.BlockSpec(kkt.shape, const)],
        out_specs=[pl.BlockSpec((hk, tb), col)] * 4,
        scratch_shapes=[pltpu.VMEM((n_cand, tb), F32)],
        compiler_params=_cparams(("parallel",)),
        name="peer_route",
    )(x1t, wqt, kkt)


def _peer_kernel(x1t_ref, u_ref, vt_ref, t_ref, c_ref, s2_ref, e2_ref, x1_ref, g2_ref, b2_ref,
                 out_ref, acc_ref, ht_ref, a_ref, *, n_heads, n_keys, alpha):
    eb_size, tb = ht_ref.shape
    eb = pl.program_id(1)
    groups = eb_size // n_keys
    chunks = tb // LANES

    @pl.when(eb == 0)
    def _():
        acc_ref[...] = jnp.zeros(acc_ref.shape, F32)

    ht_ref[...] = jnp.dot(u_ref[...], x1t_ref[...], preferred_element_type=F32)

    def block(chunk, carry):
        lanes = pl.ds(pl.multiple_of(chunk * LANES, LANES), LANES)
        t_rows = [t_ref[pl.ds(pl.multiple_of(h * n_keys + eb * groups, groups), groups), lanes]
                  for h in range(n_heads)]
        c_rows = [c_ref[pl.ds(pl.multiple_of(h * n_keys + eb * groups, groups), groups), lanes]
                  for h in range(n_heads)]
        for grp in range(groups):
            gate = jnp.zeros((n_keys, LANES), F32)
            for h in range(n_heads):
                s2 = s2_ref[h * n_keys:(h + 1) * n_keys, lanes]
                e2 = e2_ref[h * n_keys:(h + 1) * n_keys, lanes]
                gate = gate + jnp.where(s2 >= t_rows[h][grp:grp + 1, :], e2, 0.0) * c_rows[h][grp:grp + 1, :]
            rows = slice(grp * n_keys, (grp + 1) * n_keys)
            a_ref[rows, lanes] = (_gelu(ht_ref[rows, lanes]) * gate).astype(BF16)
        return carry

    lax.fori_loop(0, chunks, block, 0)
    acc_ref[...] += jnp.dot(vt_ref[...], a_ref[...], preferred_element_type=F32)

    @pl.when(eb == pl.num_programs(1) - 1)
    def _():
        r = alpha * x1_ref[...] + acc_ref[...].T
        out_ref[...] = _layer_norm(r, g2_ref[...], b2_ref[...])


def _peer_dense(x1, x1t, u_bf, vt_bf, route, g2, b2, n_heads, n_keys, alpha, tb, eb):
    n, d_model = x1.shape
    n_exp = u_bf.shape[0]
    hk = n_heads * n_keys
    tok_t = lambda i, e: (0, i)
    const = lambda i, e: (0, 0)
    return pl.pallas_call(
        functools.partial(_peer_kernel, n_heads=n_heads, n_keys=n_keys, alpha=alpha),
        out_shape=jax.ShapeDtypeStruct((n, d_model), F32),
        grid=(n // tb, n_exp // eb),
        in_specs=[
            pl.BlockSpec((d_model, tb), tok_t),
            pl.BlockSpec((eb, d_model), lambda i, e: (e, 0)),
            pl.BlockSpec((d_model, eb), lambda i, e: (0, e)),
            pl.BlockSpec((hk, tb), tok_t),
            pl.BlockSpec((hk, tb), tok_t),
            pl.BlockSpec((hk, tb), tok_t),
            pl.BlockSpec((hk, tb), tok_t),
            pl.BlockSpec((tb, d_model), lambda i, e: (i, 0)),
            pl.BlockSpec((1, d_model), const),
            pl.BlockSpec((1, d_model), const),
        ],
        out_specs=pl.BlockSpec((tb, d_model), lambda i, e: (i, 0)),
        scratch_shapes=[pltpu.VMEM((d_model, tb), F32),
                        pltpu.VMEM((eb, tb), F32),
                        pltpu.VMEM((eb, tb), BF16)],
        compiler_params=_cparams(("parallel", "arbitrary")),
        name="peer_dense",
    )(x1t, u_bf, vt_bf, *route, x1, g2, b2)


def _decoder_layer(x, pos0, h0_re, h0_im, k_buf, v_buf, p):
    bsz, t, d_model = x.shape
    n = bsz * t
    widths = p['widths']
    ssm_w, attn_w, kv_w, _ = widths
    n_kv = kv_w // HEAD_DIM
    n_heads = attn_w // HEAD_DIM
    win = p['window']
    tm = 512
    prompt = k_buf is None

    tab_len = t if prompt else tm
    cos_t, sin_t = _rope_tables([pos0 + (r % t) for r in range(tab_len)])
    x2d = x.reshape(n, d_model)
    u, q, k, v, gs, ga = _in_proj(x2d, p['w_in'], p['b_in'], cos_t, sin_t, widths, tm)

    if prompt:
        y3, h_re, h_im = _ssm_scan(u.reshape(bsz, t, ssm_w), h0_re, h0_im, p['ssm'], bsz, win)
        o2d = _attn_prompt(q, k, v, p['sinks'], bsz, t, win, n_heads, n_kv)
        k_win = k.reshape(bsz, t, n_kv, HEAD_DIM)[:, t - win:]
        v_win = v.reshape(bsz, t, n_kv, HEAD_DIM)[:, t - win:]
    else:
        y3, h_re, h_im = _ssm_scan(u.reshape(bsz, t, ssm_w), h0_re, h0_im, p['ssm'], 32, t)
        pad = jnp.zeros((bsz, win - t, kv_w), F32)
        kk = jnp.concatenate([k_buf.reshape(bsz, win, kv_w), k.reshape(bsz, t, kv_w), pad], axis=1)
        vv = jnp.concatenate([v_buf.reshape(bsz, win, kv_w), v.reshape(bsz, t, kv_w), pad], axis=1)
        o2d = _attn_sample(q.reshape(bsz, t, attn_w), kk, vv, p['sinks'], win, n_heads, n_kv, 8).reshape(n, attn_w)
        k_win = kk[:, t:t + win].reshape(bsz, win, n_kv, HEAD_DIM)
        v_win = vv[:, t:t + win].reshape(bsz, win, n_kv, HEAD_DIM)

    x1, x1t = _merge(x2d, y3.reshape(n, ssm_w), o2d, gs, ga, p['w_glu'], p['b_glu'], p['w_up'], p['w_out'],
                     p['ln1_g'], p['ln1_b'], p['alpha'], tm)
    n_ph, n_keys = p['peer_heads'], p['peer_keys']
    route = _route(x1t, p['wqt'], p['kkt'], n_ph, n_keys, 256)
    x2 = _peer_dense(x1, x1t, p['peer_u'], p['peer_vt'], route, p['ln2_g'], p['ln2_b'],
                     n_ph, n_keys, p['alpha'], 512, 8 * n_keys)
    g = ssm_w // SSM_GROUP
    return (x2.reshape(bsz, t, d_model), k_win, v_win,
            h_re.reshape(bsz, g, -1), h_im.reshape(bsz, g, -1))


def _layer_params(l, depth, window, w_in, b_in, ssm_a_re, ssm_a_im, ssm_log_dt, ssm_b_re, ssm_b_im, ssm_c_re,
                  ssm_c_im, ssm_d, w_glu, b_glu, attn_sinks, w_attn_up, w_out, ln1_g, ln1_b,
                  peer_wq, peer_k1, peer_k2, peer_u, peer_v, ln2_g, ln2_b, kv_w):
    d_model = w_in.shape[1]
    ssm_w = ssm_d.shape[1]
    attn_w = w_attn_up.shape[1]
    n_ph, n_keys, half = peer_k1.shape[1:]
    eye = jnp.eye(n_ph, dtype=F32)
    keys = jnp.stack([peer_k1[l], peer_k2[l]], axis=1)
    kkt = jnp.einsum('hckd,hg,ce->hckged', keys, eye, jnp.eye(2, dtype=F32))
    kkt = kkt.reshape(n_ph * 2 * n_keys, n_ph * 2 * half)
    row = lambda a: a[l].reshape(1, -1)
    return {
        'widths': (ssm_w, attn_w, kv_w, d_model), 'window': window,
        'alpha': (2.0 * depth) ** 0.25,
        'w_in': w_in[l].astype(BF16), 'b_in': row(b_in),
        'ssm': _ssm_params(ssm_a_re[l], ssm_a_im[l], ssm_log_dt[l], ssm_b_re[l], ssm_b_im[l],
                           ssm_c_re[l], ssm_c_im[l], ssm_d[l]),
        'w_glu': w_glu[l].astype(BF16), 'b_glu': row(b_glu), 'sinks': attn_sinks[l],
        'w_up': w_attn_up[l].astype(BF16), 'w_out': w_out[l].astype(BF16),
        'ln1_g': row(ln1_g), 'ln1_b': row(ln1_b), 'ln2_g': row(ln2_g), 'ln2_b': row(ln2_b),
        'wqt': peer_wq[l].T.astype(BF16), 'kkt': kkt.astype(BF16),
        'peer_heads': n_ph, 'peer_keys': n_keys,
        'peer_u': peer_u[l].astype(BF16), 'peer_vt': peer_v[l].T.astype(BF16),
    }


def kernel(x_prompt, x_sample, cache_k_win, cache_v_win, state_ssm_re, state_ssm_im, w_in, b_in, ssm_a_re, ssm_a_im, ssm_log_dt, ssm_b_re, ssm_b_im, ssm_c_re, ssm_c_im, ssm_d, w_glu, b_glu, attn_sinks, w_attn_up, w_out, ln1_g, ln1_b, peer_wq, peer_k1, peer_k2, peer_u, peer_v, ln2_g, ln2_b):
    depth = w_in.shape[0]
    bsz = x_prompt.shape[0]
    window, n_kv = cache_k_win.shape[2], cache_k_win.shape[3]
    ns = state_ssm_re.shape[2] * state_ssm_re.shape[3]
    zero_state = jnp.zeros((bsz, ns), F32)
    xp, xs = x_prompt, x_sample
    outs = [[] for _ in range(8)]
    for l in range(depth):
        p = _layer_params(l, depth, window, w_in, b_in, ssm_a_re, ssm_a_im, ssm_log_dt, ssm_b_re, ssm_b_im,
                          ssm_c_re, ssm_c_im, ssm_d, w_glu, b_glu, attn_sinks, w_attn_up, w_out, ln1_g, ln1_b,
                          peer_wq, peer_k1, peer_k2, peer_u, peer_v, ln2_g, ln2_b, n_kv * HEAD_DIM)
        xp, kw, vw, hr, hi = _decoder_layer(xp, 0, zero_state, zero_state, None, None, p)
        for o, a in zip(outs[:4], (kw, vw, hr, hi)):
            o.append(a)
        xs, kw, vw, hr, hi = _decoder_layer(xs, PAST_LEN, state_ssm_re[l].reshape(-1, ns),
                                            state_ssm_im[l].reshape(-1, ns), cache_k_win[l], cache_v_win[l], p)
        for o, a in zip(outs[4:], (kw, vw, hr, hi)):
            o.append(a)
    return (xp, xs) + tuple(jnp.stack(o) for o in outs)
```
